```python
import jax
import jax.numpy as jnp
from jax import lax
import numpy as np

D_MODEL = 4096
BATCH = 16
SEQ = 256
DEPTH = 2
DEC_BATCH = 8
DEC_SEQ = 4096
PAST_LEN = 256

GRID_W = 64
ROPE_THETA = 10000.0
EPS = 1e-6
NEG = -1e30
HEAD_DIM = 128
QBLK = 128
MIX_WIDTH = 4096
CONV_GROUPS = 8
CONV_DIM = CONV_GROUPS * HEAD_DIM
SWA_Q_HEADS = 24
SWA_KV_HEADS = 8
GQA_GROUP = SWA_Q_HEADS // SWA_KV_HEADS
SWA_WINDOW = 128
Q_DIM = SWA_Q_HEADS * HEAD_DIM
KV_DIM = SWA_KV_HEADS * HEAD_DIM
AB_IN = 3 * CONV_DIM + Q_DIM + 2 * KV_DIM
AB_SPLITS = (CONV_DIM, 2 * CONV_DIM, 3 * CONV_DIM, 3 * CONV_DIM + Q_DIM, 3 * CONV_DIM + Q_DIM + KV_DIM)
POOL_WINDOWS = (2, 4, 8, 16)
POOL_GROUP_DIM = 256
POOL_DIM = POOL_GROUP_DIM * len(POOL_WINDOWS)
MLA_HEADS = 24
Q_LORA = 1024
KV_LORA = 512
QK_NOPE = 128
QK_ROPE = 64
V_HEAD = 128
MLA_SCALE = (QK_NOPE + QK_ROPE) ** -0.5
CD_IN = POOL_DIM + Q_LORA + KV_LORA + QK_ROPE
CD_SPLITS = (POOL_DIM, POOL_DIM + Q_LORA, POOL_DIM + Q_LORA + KV_LORA)
N_EXPERTS = 32
TOP_K = 4
EXPERT_DIM = 2048
SWIGLU_LIMIT = 7.0
SWIGLU_ALPHA = 1.702
MOE_BLOCK = 512
N_EVEN = (DEPTH + 1) // 2
N_ODD = DEPTH // 2

kernel_name = "hybrid_diffusion_trunk_step"


def rms_norm(x, g):
    xf = x.astype(jnp.float32)
    y = xf * lax.rsqrt(jnp.mean(xf * xf, axis=-1, keepdims=True) + EPS)
    return (y * g.astype(jnp.float32)).astype(x.dtype)


def modulation(cvec, w_ada, b_ada):
    m = (jax.nn.silu(cvec) @ w_ada + b_ada)[..., None, :]
    return jnp.split(m, 6, axis=-1)


def modulate(x, g, shift, scale):
    return rms_norm(x, g) * (1 + scale) + shift


def gated_residual(x, y, g, gate):
    return x + gate * rms_norm(y, g)


def grid_angles(n_tok, rot_dim):
    rows = n_tok // GRID_W
    row = jnp.repeat(jnp.arange(rows), GRID_W).astype(jnp.float32)
    col = jnp.tile(jnp.arange(GRID_W), rows).astype(jnp.float32)
    n_freq = rot_dim // 4
    inv = ROPE_THETA ** (-jnp.arange(n_freq, dtype=jnp.float32) / n_freq)
    return row[:, None] * inv, col[:, None] * inv


def _rotate_half(x, ang):
    n = x.shape[-1] // 2
    c = jnp.cos(ang)[:, None, :]
    s = jnp.sin(ang)[:, None, :]
    a, b = x[..., :n], x[..., n:]
    return jnp.concatenate([a * c - b * s, b * c + a * s], axis=-1)


def axial_rope(x):
    S, R = x.shape[1], x.shape[-1]
    ang_r, ang_c = grid_angles(S, R)
    xf = x.astype(jnp.float32)
    out = jnp.concatenate([_rotate_half(xf[..., :R // 2], ang_r), _rotate_half(xf[..., R // 2:], ang_c)], axis=-1)
    return out.astype(x.dtype)


def to_blocks(a):
    B, S = a.shape[:2]
    return jnp.moveaxis(a.reshape((B, S // QBLK, QBLK) + a.shape[2:]), 1, 0)


def from_blocks(o):
    o = jnp.moveaxis(o, 0, 1)
    return o.reshape((o.shape[0], -1) + o.shape[3:])


def short_conv_mixer(u_b, u_c, u_x, conv_w):
    h = u_c * u_x
    hp = jnp.pad(h, ((0, 0), (1, 1), (0, 0)))
    h = hp[:, :-2] * conv_w[0] + hp[:, 1:-1] * conv_w[1] + hp[:, 2:] * conv_w[2]
    return u_b * h


def gqa_sink_block(qi, kb, vb, valid, sink):
    s = jnp.einsum('bqhgd,blhd->bhgql', qi, kb).astype(jnp.float32) * (HEAD_DIM ** -0.5)
    s = jnp.where(valid, s, NEG)
    sk = sink.astype(jnp.float32).reshape(SWA_KV_HEADS, GQA_GROUP)[None, :, :, None, None]
    sk = jnp.broadcast_to(sk, s.shape[:-1] + (1,))
    p = jax.nn.softmax(jnp.concatenate([s, sk], axis=-1), axis=-1)[..., :-1]
    return jnp.einsum('bhgql,blhd->bqhgd', p.astype(vb.dtype), vb)


def swa_context(q, k, v, sink):
    B, L = k.shape[:2]
    qb = to_blocks(q.reshape(B, L, SWA_KV_HEADS, GQA_GROUP, HEAD_DIM))
    valid = jnp.ones((QBLK, L), dtype=bool)
    o = lax.map(lambda qi: gqa_sink_block(qi, k, v, valid, sink), qb)
    return from_blocks(o).reshape(B, L, Q_DIM)


def swa_latent(q, k, v, k_ctx, v_ctx, sink):
    B, S = q.shape[:2]
    Lc = k_ctx.shape[1]
    qb = to_blocks(q.reshape(B, S, SWA_KV_HEADS, GQA_GROUP, HEAD_DIM))
    pad = ((0, 0), (QBLK, QBLK), (0, 0), (0, 0))
    kp, vp = jnp.pad(k, pad), jnp.pad(v, pad)
    ctx_valid = jnp.ones((QBLK, Lc), dtype=bool)

    def block(args):
        i, qi = args
        start = i * QBLK
        kb = lax.dynamic_slice_in_dim(kp, start, 3 * QBLK, axis=1)
        vb = lax.dynamic_slice_in_dim(vp, start, 3 * QBLK, axis=1)
        qpos = start + jnp.arange(QBLK)
        kpos = start - QBLK + jnp.arange(3 * QBLK)
        band = (jnp.abs(qpos[:, None] - kpos[None, :]) <= SWA_WINDOW) & (kpos >= 0)[None, :] & (kpos < S)[None, :]
        valid = jnp.concatenate([ctx_valid, band], axis=1)
        return gqa_sink_block(qi, jnp.concatenate([k_ctx, kb], 1), jnp.concatenate([v_ctx, vb], 1), valid, sink)

    o = lax.map(block, (jnp.arange(S // QBLK), qb))
    return from_blocks(o).reshape(B, S, Q_DIM)


def mix_ab_context(h, w_in, conv_w, sink, w_out):
    B, L, _ = h.shape
    ub, uc, ux, q, k, v = jnp.split(h @ w_in, AB_SPLITS, axis=-1)
    ya = short_conv_mixer(ub, uc, ux, conv_w)
    k = k.reshape(B, L, SWA_KV_HEADS, HEAD_DIM)
    v = v.reshape(B, L, SWA_KV_HEADS, HEAD_DIM)
    yb = swa_context(q.reshape(B, L, SWA_Q_HEADS, HEAD_DIM), k, v, sink)
    return jnp.concatenate([ya, yb], axis=-1) @ w_out, k, v


def mix_ab_latent(h, k_ctx, v_ctx, w_in, conv_w, sink, w_out):
    B, S, _ = h.shape
    ub, uc, ux, q, k, v = jnp.split(h @ w_in, AB_SPLITS, axis=-1)
    ya = short_conv_mixer(ub, uc, ux, conv_w)
    q = axial_rope(q.reshape(B, S, SWA_Q_HEADS, HEAD_DIM))
    k = axial_rope(k.reshape(B, S, SWA_KV_HEADS, HEAD_DIM))
    v = v.reshape(B, S, SWA_KV_HEADS, HEAD_DIM)
    yb = swa_latent(q, k, v, k_ctx, v_ctx, sink)
    return jnp.concatenate([ya, yb], axis=-1) @ w_out


def multiscale_pool(u, w_pool, pool_scale):
    B, S, _ = u.shape
    G = len(POOL_WINDOWS)
    ug = u.reshape(B, S, G, POOL_GROUP_DIM)
    cs = jnp.concatenate([jnp.zeros((B, 1, G, POOL_GROUP_DIM), jnp.float32),
                          jnp.cumsum(ug.astype(jnp.float32), axis=1)], axis=1)
    t = jnp.arange(S)
    means = []
    for gi, w in enumerate(POOL_WINDOWS):
        lo = jnp.clip(t - w // 2, 0, S)
        hi = jnp.clip(t + w - w // 2, 0, S)
        means.append((cs[:, hi, gi] - cs[:, lo, gi]) / (hi - lo).astype(jnp.float32)[None, :, None])
    pooled = jnp.stack(means, axis=2).astype(u.dtype) - ug
    y = jnp.einsum('bsgc,gcd->bsgd', pooled, w_pool)
    return y.reshape(B, S, POOL_DIM) * pool_scale


def mla_project(h, w_in, q_a_norm, w_qb, kv_a_norm):
    B, S, _ = h.shape
    u, qa, kva, kpe = jnp.split(h @ w_in, CD_SPLITS, axis=-1)
    q = (rms_norm(qa, q_a_norm) @ w_qb).reshape(B, S, MLA_HEADS, QK_NOPE + QK_ROPE)
    ckv = rms_norm(kva, kv_a_norm)
    return u, q[..., :QK_NOPE], q[..., QK_NOPE:], ckv, kpe


def mla_block(ql, qp, ckv, kpe):
    s = (jnp.einsum('bqhc,blc->bhql', ql, ckv) + jnp.einsum('bqhr,blr->bhql', qp, kpe)).astype(jnp.float32) * MLA_SCALE
    p = jax.nn.softmax(s, axis=-1)
    return jnp.einsum('bhql,blc->bqhc', p.astype(ckv.dtype), ckv)


def mla_output(u, q_nope, q_pe, ckv_all, kpe_all, w_kvb, w_pool, pool_scale, w_out):
    B, S = u.shape[:2]
    wkv = w_kvb.reshape(KV_LORA, MLA_HEADS, QK_NOPE + V_HEAD)
    q_lat = jnp.einsum('bshd,chd->bshc', q_nope, wkv[..., :QK_NOPE])
    o = lax.map(lambda a: mla_block(a[0], a[1], ckv_all, kpe_all), (to_blocks(q_lat), to_blocks(q_pe)))
    o = from_blocks(o)
    yd = jnp.einsum('bshc,chd->bshd', o, wkv[..., QK_NOPE:]).reshape(B, S, MLA_HEADS * V_HEAD)
    yc = multiscale_pool(u, w_pool, pool_scale)
    return jnp.concatenate([yc, yd], axis=-1) @ w_out


def moe(h, w_router, b_router, w_gu, b_gu, w_dn, b_dn):
    B, S, D = h.shape
    xt = h.reshape(-1, D)
    T = xt.shape[0]
    logits = (xt @ w_router + b_router).astype(jnp.float32)
    top_logits, top_idx = lax.top_k(logits, TOP_K)
    top_w = jax.nn.softmax(top_logits, axis=-1).astype(h.dtype)
    A = T * TOP_K
    flat_e = top_idx.reshape(-1)
    flat_tok = (jnp.arange(A) // TOP_K).astype(jnp.int32)
    flat_w = top_w.reshape(-1)
    order = jnp.argsort(flat_e)
    sorted_e = flat_e[order]
    counts = jnp.bincount(flat_e, length=N_EXPERTS)
    padded = (counts + MOE_BLOCK - 1) // MOE_BLOCK * MOE_BLOCK
    pad_end = jnp.cumsum(padded)
    pad_start = pad_end - padded
    start = jnp.cumsum(counts) - counts
    dest = pad_start[sorted_e] + jnp.arange(A) - start[sorted_e]
    n_blocks = -(-A // MOE_BLOCK) + N_EXPERTS
    P = n_blocks * MOE_BLOCK
    src = jnp.full((P,), T, jnp.int32).at[dest].set(flat_tok[order])
    wts = jnp.zeros((P,), h.dtype).at[dest].set(flat_w[order])
    blk_e = jnp.minimum(jnp.searchsorted(pad_end, jnp.arange(n_blocks) * MOE_BLOCK, side='right'), N_EXPERTS - 1)
    x_pad = jnp.concatenate([xt, jnp.zeros((1, D), h.dtype)], axis=0)

    def step(out, args):
        e, s_idx, w_blk = args
        xb = x_pad[s_idx]
        gu = xb @ w_gu[e] + b_gu[e]
        glu = jnp.minimum(gu[:, ::2], SWIGLU_LIMIT)
        lin = jnp.clip(gu[:, 1::2], -SWIGLU_LIMIT, SWIGLU_LIMIT)
        act = glu * jax.nn.sigmoid(SWIGLU_ALPHA * glu) * (lin + 1)
        yb = act @ w_dn[e] + b_dn[e]
        return out.at[s_idx].add(yb * w_blk[:, None]), None

    out, _ = lax.scan(step, jnp.zeros((T + 1, D), h.dtype),
                      (blk_e, src.reshape(n_blocks, MOE_BLOCK), wts.reshape(n_blocks, MOE_BLOCK)))
    return out[:T].reshape(B, S, D)


def setup_inputs(seed: int = 0) -> dict:
    key = jax.random.key(seed)
    ks = list(jax.random.split(key, 40))
    D = D_MODEL

    def nrm(shape, scale):
        return jax.random.normal(ks.pop(), shape, jnp.float32) * scale

    def gain(shape):
        return 1.0 + nrm(shape, 0.05)

    return {
        "x_prompt": nrm((BATCH, SEQ, D), 1.0),
        "x_sample": nrm((DEC_BATCH, DEC_SEQ, D), 1.0),
        "cache_swa_k": nrm((DEC_BATCH, N_EVEN, PAST_LEN, SWA_KV_HEADS, HEAD_DIM), 1.0),
        "cache_swa_v": nrm((DEC_BATCH, N_EVEN, PAST_LEN, SWA_KV_HEADS, HEAD_DIM), 1.0),
        "cache_mla_ckv": nrm((DEC_BATCH, N_ODD, PAST_LEN, KV_LORA), 1.0),
        "cache_mla_kpe": nrm((DEC_BATCH, N_ODD, PAST_LEN, QK_ROPE), 1.0),
        "c": nrm((DEC_BATCH, D), 1.0),
        "c_ctx": nrm((D,), 1.0),
        "w_ada": nrm((DEPTH, D, 6 * D), 0.5 * D ** -0.5),
        "b_ada": nrm((DEPTH, 6 * D), 0.02),
        "g_pre_mix": gain((DEPTH, D)),
        "g_post_mix": gain((DEPTH, D)),
        "g_pre_ffn": gain((DEPTH, D)),
        "g_post_ffn": gain((DEPTH, D)),
        "w_in_ab": nrm((N_EVEN, D, AB_IN), D ** -0.5),
        "conv_w": nrm((N_EVEN, 3, CONV_DIM), 0.5),
        "sink": nrm((N_EVEN, SWA_Q_HEADS), 1.0),
        "w_out_ab": nrm((N_EVEN, MIX_WIDTH, D), MIX_WIDTH ** -0.5),
        "w_in_cd": nrm((N_ODD, D, CD_IN), D ** -0.5),
        "q_a_norm": gain((N_ODD, Q_LORA)),
        "w_qb": nrm((N_ODD, Q_LORA, MLA_HEADS * (QK_NOPE + QK_ROPE)), Q_LORA ** -0.5),
        "kv_a_norm": gain((N_ODD, KV_LORA)),
        "w_kvb": nrm((N_ODD, KV_LORA, MLA_HEADS * (QK_NOPE + V_HEAD)), KV_LORA ** -0.5),
        "w_pool": nrm((N_ODD, len(POOL_WINDOWS), POOL_GROUP_DIM, POOL_GROUP_DIM), POOL_GROUP_DIM ** -0.5),
        "pool_scale": 0.5 + nrm((N_ODD, POOL_DIM), 0.1),
        "w_out_cd": nrm((N_ODD, MIX_WIDTH, D), MIX_WIDTH ** -0.5),
        "w_router": nrm((DEPTH, D, N_EXPERTS), D ** -0.5),
        "b_router": nrm((DEPTH, N_EXPERTS), 0.01),
        "w_gu": nrm((DEPTH, N_EXPERTS, D, 2 * EXPERT_DIM), D ** -0.5),
        "b_gu": nrm((DEPTH, N_EXPERTS, 2 * EXPERT_DIM), 0.02),
        "w_dn": nrm((DEPTH, N_EXPERTS, EXPERT_DIM, D), EXPERT_DIM ** -0.5),
        "b_dn": nrm((DEPTH, N_EXPERTS, D), 0.02),
    }


def reference(x_prompt, x_sample, cache_swa_k, cache_swa_v, cache_mla_ckv, cache_mla_kpe, c, c_ctx,
              w_ada, b_ada, g_pre_mix, g_post_mix, g_pre_ffn, g_post_ffn,
              w_in_ab, conv_w, sink, w_out_ab,
              w_in_cd, q_a_norm, w_qb, kv_a_norm, w_kvb, w_pool, pool_scale, w_out_cd,
              w_router, b_router, w_gu, b_gu, w_dn, b_dn):
    xc, xl = x_prompt, x_sample
    new_k, new_v, new_ckv, new_kpe = [], [], [], []
    for l in range(DEPTH):
        j = l // 2
        sh1c, sc1c, gt1c, sh2c, sc2c, gt2c = modulation(c_ctx, w_ada[l], b_ada[l])
        sh1l, sc1l, gt1l, sh2l, sc2l, gt2l = modulation(c, w_ada[l], b_ada[l])
        hc = modulate(xc, g_pre_mix[l], sh1c, sc1c)
        hl = modulate(xl, g_pre_mix[l], sh1l, sc1l)
        if l % 2 == 0:
            yc, kc, vc = mix_ab_context(hc, w_in_ab[j], conv_w[j], sink[j], w_out_ab[j])
            new_k.append(kc)
            new_v.append(vc)
            yl = mix_ab_latent(hl, cache_swa_k[:, j], cache_swa_v[:, j], w_in_ab[j], conv_w[j], sink[j], w_out_ab[j])
        else:
            uc, qnc, qpc, ckvc, kpec = mla_project(hc, w_in_cd[j], q_a_norm[j], w_qb[j], kv_a_norm[j])
            new_ckv.append(ckvc)
            new_kpe.append(kpec)
            yc = mla_output(uc, qnc, qpc, ckvc, kpec, w_kvb[j], w_pool[j], pool_scale[j], w_out_cd[j])
            ul, qnl, qpl, ckvl, kpel = mla_project(hl, w_in_cd[j], q_a_norm[j], w_qb[j], kv_a_norm[j])
            qpl = axial_rope(qpl)
            kpel = axial_rope(kpel[:, :, None, :])[:, :, 0]
            ckv_all = jnp.concatenate([cache_mla_ckv[:, j], ckvl], axis=1)
            kpe_all = jnp.concatenate([cache_mla_kpe[:, j], kpel], axis=1)
            yl = mla_output(ul, qnl, qpl, ckv_all, kpe_all, w_kvb[j], w_pool[j], pool_scale[j], w_out_cd[j])
        xc = gated_residual(xc, yc, g_post_mix[l], gt1c)
        xl = gated_residual(xl, yl, g_post_mix[l], gt1l)
        hc = modulate(xc, g_pre_ffn[l], sh2c, sc2c)
        hl = modulate(xl, g_pre_ffn[l], sh2l, sc2l)
        fc = moe(hc, w_router[l], b_router[l], w_gu[l], b_gu[l], w_dn[l], b_dn[l])
        fl = moe(hl, w_router[l], b_router[l], w_gu[l], b_gu[l], w_dn[l], b_dn[l])
        xc = gated_residual(xc, fc, g_post_ffn[l], gt2c)
        xl = gated_residual(xl, fl, g_post_ffn[l], gt2l)
    return (xc, xl, jnp.stack(new_k, axis=1), jnp.stack(new_v, axis=1), jnp.stack(new_ckv, axis=1), jnp.stack(new_kpe, axis=1))
```

```python
from functools import partial

import jax
import jax.numpy as jnp
from jax import lax
from jax.experimental import pallas as pl
from jax.experimental.pallas import tpu as pltpu

GRID_W = 64
ROPE_THETA = 10000.0
EPS = 1e-6
NEG = -1e30
HEAD_DIM = 128
QBLK = 128
CONV_GROUPS = 8
SWA_Q_HEADS = 24
SWA_KV_HEADS = 8
SWA_WINDOW = 128
POOL_WINDOWS = (2, 4, 8, 16)
POOL_GROUP_DIM = 256
MLA_HEADS = 24
Q_LORA = 1024
KV_LORA = 512
QK_NOPE = 128
QK_ROPE = 64
V_HEAD = 128
N_EXPERTS = 32
TOP_K = 4
EXPERT_DIM = 2048
SWIGLU_LIMIT = 7.0
SWIGLU_ALPHA = 1.702

GQA_GROUP = SWA_Q_HEADS // SWA_KV_HEADS
CONV_DIM = CONV_GROUPS * HEAD_DIM
Q_DIM = SWA_Q_HEADS * HEAD_DIM
KV_DIM = SWA_KV_HEADS * HEAD_DIM
POOL_DIM = POOL_GROUP_DIM * len(POOL_WINDOWS)
MLA_SCALE = (QK_NOPE + QK_ROPE) ** -0.5
MLA_QW = 2 * QK_NOPE
POOL_PAD = 16

LANE = 128
VMEM_LIMIT = 56 * 1024 * 1024
ROW_TILE = 512
EW_TILE = 256
MOE_TM = 512

F32 = jnp.float32
BF16 = jnp.bfloat16


def _cparams(sem):
    return pltpu.CompilerParams(dimension_semantics=sem, vmem_limit_bytes=VMEM_LIMIT)


def _rms(x, g):
    return x * lax.rsqrt(jnp.mean(x * x, axis=-1, keepdims=True) + EPS) * g


def _ada_kernel(c_ref, w_ref, b_ref, o_ref):
    k = pl.program_id(2)

    @pl.when(k == 0)
    def _():
        o_ref[0] = jnp.broadcast_to(b_ref[0], o_ref.shape[1:])

    c = c_ref[...]
    a = (c * jax.nn.sigmoid(c)).astype(BF16)
    o_ref[0] += jnp.dot(a, w_ref[0].astype(BF16), preferred_element_type=F32)


def _ada(cond, w_ada, b_ada):
    depth, d, n = w_ada.shape
    m = cond.shape[0]
    tk, tn = min(512, d), min(2048, n)
    return pl.pallas_call(
        _ada_kernel,
        grid=(depth, n // tn, d // tk),
        in_specs=[pl.BlockSpec((m, tk), lambda l, j, k: (0, k)),
                  pl.BlockSpec((1, tk, tn), lambda l, j, k: (l, k, j)),
                  pl.BlockSpec((1, 1, tn), lambda l, j, k: (l, 0, j))],
        out_specs=pl.BlockSpec((1, m, tn), lambda l, j, k: (l, 0, j)),
        out_shape=jax.ShapeDtypeStruct((depth, m, n), F32),
        compiler_params=_cparams(("parallel", "parallel", "arbitrary")),
        name="ada",
    )(cond, w_ada, b_ada.reshape(depth, 1, n))


def _mm_kernel(*refs, norm, mod):
    x_ref = refs[0]
    pos = 1
    if norm:
        g_ref = refs[pos]
        pos += 1
    if mod:
        sc_ref, sh_ref = refs[pos], refs[pos + 1]
        pos += 2
    w_ref, o_ref = refs[pos], refs[pos + 1]
    if norm:
        h_ref = refs[pos + 2]

        @pl.when(pl.program_id(1) == 0)
        def _():
            h = _rms(x_ref[...].astype(F32), g_ref[...])
            if mod:
                h = h * (1.0 + sc_ref[0]) + sh_ref[0]
            h_ref[...] = h.astype(BF16)

        lhs = h_ref[...]
    else:
        lhs = x_ref[...]
    o_ref[...] = jnp.dot(lhs, w_ref[...], preferred_element_type=F32).astype(o_ref.dtype)


def _mm(x, w, *, tn, out_dtype, g=None, mods=None, mod_cols=None, mod_index=None, x_col=0, tm=ROW_TILE):
    t = x.shape[0]
    k, n = w.shape
    norm, mod = g is not None, mods is not None
    in_specs = [pl.BlockSpec((tm, k), lambda i, j: (i, x_col))]
    args = [x]
    if norm:
        in_specs.append(pl.BlockSpec((1, k), lambda i, j: (0, 0)))
        args.append(g.reshape(1, k))
    if mod:
        sc_col, sh_col = mod_cols
        in_specs.append(pl.BlockSpec((1, 1, k), lambda i, j: (mod_index(i), 0, sc_col)))
        in_specs.append(pl.BlockSpec((1, 1, k), lambda i, j: (mod_index(i), 0, sh_col)))
        args += [mods, mods]
    in_specs.append(pl.BlockSpec((k, tn), lambda i, j: (0, j)))
    args.append(w)
    return pl.pallas_call(
        partial(_mm_kernel, norm=norm, mod=mod),
        grid=(t // tm, n // tn),
        in_specs=in_specs,
        out_specs=pl.BlockSpec((tm, tn), lambda i, j: (i, j)),
        out_shape=jax.ShapeDtypeStruct((t, n), out_dtype),
        scratch_shapes=[pltpu.VMEM((tm, k), BF16)] if norm else [],
        compiler_params=_cparams(("parallel", "arbitrary")),
        name="mm",
    )(*args)


def _resid_kernel(*refs, with_h):
    x_ref, y_ref, gate_ref, gpost_ref = refs[:4]
    if with_h:
        gpre_ref, sc_ref, sh_ref, xo_ref, ho_ref = refs[4:]
    else:
        xo_ref = refs[4]
    xn = x_ref[...] + gate_ref[0] * _rms(y_ref[...].astype(F32), gpost_ref[...])
    xo_ref[...] = xn
    if with_h:
        ho_ref[...] = (_rms(xn, gpre_ref[...]) * (1.0 + sc_ref[0]) + sh_ref[0]).astype(BF16)


def _resid(x, y, mods, mod_index, gate_col, gpost, gpre=None, mod_cols=None, tm=EW_TILE):
    t, d = x.shape
    with_h = gpre is not None
    row = pl.BlockSpec((tm, d), lambda i: (i, 0))
    vec = pl.BlockSpec((1, d), lambda i: (0, 0))

    def modspec(col):
        return pl.BlockSpec((1, 1, d), lambda i: (mod_index(i), 0, col))

    in_specs = [row, row, modspec(gate_col), vec]
    args = [x, y, mods, gpost.reshape(1, d)]
    out_specs = [row]
    out_shape = [jax.ShapeDtypeStruct((t, d), F32)]
    if with_h:
        in_specs += [vec, modspec(mod_cols[0]), modspec(mod_cols[1])]
        args += [gpre.reshape(1, d), mods, mods]
        out_specs.append(row)
        out_shape.append(jax.ShapeDtypeStruct((t, d), BF16))
    out = pl.pallas_call(
        partial(_resid_kernel, with_h=with_h),
        grid=(t // tm,),
        in_specs=in_specs, out_specs=out_specs, out_shape=out_shape,
        compiler_params=_cparams(("parallel",)),
        name="resid",
    )(*args)
    return out if with_h else out[0]


def _swap_halves(x, half):
    w = x.shape[1]
    lane = lax.broadcasted_iota(jnp.int32, x.shape, 1)
    return jnp.where(lane % (2 * half) < half, pltpu.roll(x, w - half, axis=1), pltpu.roll(x, half, axis=1))


def _rope_kernel(x_ref, cos_ref, sin_ref, o_ref, *, half):
    x = x_ref[...]
    rep = x.shape[1] // cos_ref.shape[1]
    cos = jnp.concatenate([cos_ref[...]] * rep, axis=1)
    sin = jnp.concatenate([sin_ref[...]] * rep, axis=1)
    o_ref[...] = (x * cos + _swap_halves(x, half) * sin).astype(o_ref.dtype)


def _rope(x, cos, sin, *, col0, width, half, tw, tm=ROW_TILE):
    t = x.shape[0]
    pw = cos.shape[1]
    return pl.pallas_call(
        partial(_rope_kernel, half=half),
        grid=(t // tm, width // tw),
        in_specs=[pl.BlockSpec((tm, tw), lambda i, j: (i, col0 + j)),
                  pl.BlockSpec((tm, pw), lambda i, j: (i, 0)),
                  pl.BlockSpec((tm, pw), lambda i, j: (i, 0))],
        out_specs=pl.BlockSpec((tm, tw), lambda i, j: (i, j)),
        out_shape=jax.ShapeDtypeStruct((t, width), BF16),
        compiler_params=_cparams(("parallel", "arbitrary")),
        name="rope",
    )(x, cos, sin)


def _rope_tables(n_ctx, n_lat_seq, n_lat_batch, rot_dim):
    pos = jnp.arange(n_lat_seq)
    row = (pos // GRID_W).astype(F32)
    col = (pos % GRID_W).astype(F32)
    n_freq = rot_dim // 4
    inv = ROPE_THETA ** (-jnp.arange(n_freq, dtype=F32) / n_freq)
    ar, ac = row[:, None] * inv, col[:, None] * inv
    cos = jnp.concatenate([jnp.cos(ar), jnp.cos(ar), jnp.cos(ac), jnp.cos(ac)], axis=-1)
    sin = jnp.concatenate([-jnp.sin(ar), jnp.sin(ar), -jnp.sin(ac), jnp.sin(ac)], axis=-1)
    pad = LANE - rot_dim
    cos = jnp.pad(cos, ((0, 0), (0, pad)), constant_values=1.0)
    sin = jnp.pad(sin, ((0, 0), (0, pad)))
    cos = jnp.concatenate([jnp.ones((n_ctx, LANE), F32), jnp.tile(cos, (n_lat_batch, 1))], axis=0)
    sin = jnp.concatenate([jnp.zeros((n_ctx, LANE), F32), jnp.tile(sin, (n_lat_batch, 1))], axis=0)
    return cos, sin


def _conv_kernel(b_ref, c_ref, x_ref, w_ref, o_ref):
    h = c_ref[...] * x_ref[...]
    s = h.shape[0]
    row = lax.broadcasted_iota(jnp.int32, h.shape, 0)
    prev = jnp.where(row >= 1, pltpu.roll(h, 1, axis=0), 0.0)
    nxt = jnp.where(row < s - 1, pltpu.roll(h, s - 1, axis=0), 0.0)
    w = w_ref[...]
    o_ref[...] = (b_ref[...] * (prev * w[0:1] + h * w[1:2] + nxt * w[2:3])).astype(o_ref.dtype)


def _conv(u, conv_w, *, seq, nseq, row0):
    ncb = CONV_DIM // LANE

    def spec(off):
        return pl.BlockSpec((seq, LANE), lambda b, c: (row0 + b, off + c))

    return pl.pallas_call(
        _conv_kernel,
        grid=(nseq, ncb),
        in_specs=[spec(0), spec(ncb), spec(2 * ncb), pl.BlockSpec((3, LANE), lambda b, c: (0, c))],
        out_specs=pl.BlockSpec((seq, LANE), lambda b, c: (b, c)),
        out_shape=jax.ShapeDtypeStruct((nseq * seq, CONV_DIM), BF16),
        compiler_params=_cparams(("parallel", "parallel")),
        name="conv",
    )(u, u, u, conv_w)


def _gqa_kernel(*refs, masks, nq):
    npiece = len(masks)
    q_ref, sink_ref = refs[0], refs[1]
    k_refs = refs[2:2 + npiece]
    v_refs = refs[2 + npiece:2 + 2 * npiece]
    o_ref = refs[2 + 2 * npiece]
    i = pl.program_id(2)
    q = q_ref[...]
    tq = q.shape[0]
    qs = jnp.concatenate([q[:, g * HEAD_DIM:(g + 1) * HEAD_DIM] for g in range(GQA_GROUP)], axis=0)
    scale = HEAD_DIM ** -0.5
    scores = []
    for kr, mask in zip(k_refs, masks):
        k = kr[...].reshape(kr.shape[-2:]).astype(BF16)
        s = lax.dot_general(qs, k, (((1,), (1,)), ((), ())), preferred_element_type=F32) * scale
        if mask != "all":
            qr = lax.broadcasted_iota(jnp.int32, s.shape, 0) % tq
            kc = lax.broadcasted_iota(jnp.int32, s.shape, 1)
            if mask == "prev":
                ok = (kc >= qr) & (i > 0)
            else:
                ok = (kc <= qr) & (i < nq - 1)
            s = jnp.where(ok, s, NEG)
        scores.append(s)
    sink = sink_ref[0]
    m = sink
    for s in scores:
        m = jnp.maximum(m, jnp.max(s, axis=-1, keepdims=True))
    den = jnp.exp(sink - m)
    acc = jnp.zeros((qs.shape[0], HEAD_DIM), F32)
    for s, vr in zip(scores, v_refs):
        p = jnp.exp(s - m)
        den = den + jnp.sum(p, axis=-1, keepdims=True)
        v = vr[...].reshape(vr.shape[-2:]).astype(BF16)
        acc = acc + jnp.dot(p.astype(BF16), v, preferred_element_type=F32)
    o = acc / den
    o_ref[...] = jnp.concatenate([o[g * tq:(g + 1) * tq] for g in range(GQA_GROUP)], axis=1).astype(o_ref.dtype)


def _sink_rows(sink, tq):
    s = jnp.repeat(sink.astype(F32).reshape(SWA_KV_HEADS, GQA_GROUP), tq, axis=1)
    return s.reshape(SWA_KV_HEADS, GQA_GROUP * tq, 1)


def _swa_context(qk, u, sink, *, seq, nseq):
    qw = GQA_GROUP * HEAD_DIM
    kcol, vcol = Q_DIM // LANE, (3 * CONV_DIM + Q_DIM + KV_DIM) // LANE
    return pl.pallas_call(
        partial(_gqa_kernel, masks=("all",), nq=1),
        grid=(nseq, SWA_KV_HEADS, 1),
        in_specs=[pl.BlockSpec((seq, qw), lambda b, h, i: (b, h)),
                  pl.BlockSpec((1, GQA_GROUP * seq, 1), lambda b, h, i: (h, 0, 0)),
                  pl.BlockSpec((seq, LANE), lambda b, h, i: (b, kcol + h)),
                  pl.BlockSpec((seq, LANE), lambda b, h, i: (b, vcol + h))],
        out_specs=pl.BlockSpec((seq, qw), lambda b, h, i: (b, h)),
        out_shape=jax.ShapeDtypeStruct((nseq * seq, Q_DIM), BF16),
        compiler_params=_cparams(("parallel", "parallel", "arbitrary")),
        name="swa_ctx",
    )(qk, _sink_rows(sink, seq), qk, u)


def _swa_latent(qk, u, k_ctx, v_ctx, sink, *, seq, nseq, row0):
    qw = GQA_GROUP * HEAD_DIM
    nq = seq // QBLK
    base = row0 * nq
    kcol, vcol = Q_DIM // LANE, (3 * CONV_DIM + Q_DIM + KV_DIM) // LANE
    past = k_ctx.shape[1]

    def band(col, shift):
        def index(b, h, i):
            return (base + b * nq + jnp.clip(i + shift, 0, nq - 1), col + h)
        return pl.BlockSpec((QBLK, LANE), index)

    ctx = pl.BlockSpec((1, past, LANE), lambda b, h, i: (b, 0, h))
    return pl.pallas_call(
        partial(_gqa_kernel, masks=("all", "prev", "all", "next"), nq=nq),
        grid=(nseq, SWA_KV_HEADS, nq),
        in_specs=[pl.BlockSpec((QBLK, qw), lambda b, h, i: (base + b * nq + i, h)),
                  pl.BlockSpec((1, GQA_GROUP * QBLK, 1), lambda b, h, i: (h, 0, 0)),
                  ctx, band(kcol, -1), band(kcol, 0), band(kcol, 1),
                  ctx, band(vcol, -1), band(vcol, 0), band(vcol, 1)],
        out_specs=pl.BlockSpec((QBLK, qw), lambda b, h, i: (b * nq + i, h)),
        out_shape=jax.ShapeDtypeStruct((nseq * seq, Q_DIM), BF16),
        compiler_params=_cparams(("parallel", "parallel", "arbitrary")),
        name="swa_lat",
    )(qk, _sink_rows(sink, QBLK), k_ctx, qk, qk, qk, v_ctx, u, u, u)


def _pool_kernel(u_ref, w_ref, ps_ref, o_ref, pad_ref):
    s, c = u_ref.shape
    g = pl.program_id(1)
    u = u_ref[...]
    zeros = jnp.zeros((POOL_PAD, c), F32)
    pad_ref[0:POOL_PAD, :] = zeros
    pad_ref[POOL_PAD:POOL_PAD + s, :] = u
    pad_ref[POOL_PAD + s:, :] = zeros
    n = s + 2 * POOL_PAD
    t = lax.broadcasted_iota(jnp.int32, (s, 1), 0)
    wmat = w_ref[0].astype(BF16)
    for gi, win in enumerate(POOL_WINDOWS):
        @pl.when(g == gi)
        def _(win=win):
            f = pad_ref[...]
            d = 1
            while d < win:
                f = f + pltpu.roll(f, n - d, axis=0)
                d *= 2
            wsum = pltpu.roll(f, win // 2, axis=0)[POOL_PAD:POOL_PAD + s]
            cnt = jnp.minimum(t + (win - win // 2), s) - jnp.maximum(t - win // 2, 0)
            pooled = wsum / cnt.astype(F32) - u
            y = jnp.dot(pooled.astype(BF16), wmat, preferred_element_type=F32)
            o_ref[...] = (y * ps_ref[...]).astype(o_ref.dtype)


def _pool(u, w_pool, pool_scale, *, seq, nseq, row0):
    ng = len(POOL_WINDOWS)
    gd = POOL_GROUP_DIM
    return pl.pallas_call(
        _pool_kernel,
        grid=(nseq, ng),
        in_specs=[pl.BlockSpec((seq, gd), lambda b, g: (row0 + b, g)),
                  pl.BlockSpec((1, gd, gd), lambda b, g: (g, 0, 0)),
                  pl.BlockSpec((1, gd), lambda b, g: (0, g))],
        out_specs=pl.BlockSpec((seq, gd), lambda b, g: (b, g)),
        out_shape=jax.ShapeDtypeStruct((nseq * seq, POOL_DIM), BF16),
        scratch_shapes=[pltpu.VMEM((seq + 2 * POOL_PAD, gd), F32)],
        compiler_params=_cparams(("parallel", "arbitrary")),
        name="pool",
    )(u, w_pool, pool_scale.reshape(1, POOL_DIM))


def _kvprep_kernel(kva_ref, kpe_ref, g_ref, cos_ref, sin_ref, ckv_ref, kpe_o_ref, *, half):
    ckv_ref[...] = _rms(kva_ref[...], g_ref[...])
    x = kpe_ref[...]
    kpe_o_ref[...] = x * cos_ref[...] + _swap_halves(x, half) * sin_ref[...]


def _kvprep(ucd, kv_a_norm, cos, sin, tm=ROW_TILE):
    t = ucd.shape[0]
    kva_blk = (POOL_DIM + Q_LORA) // KV_LORA
    kpe_blk = (POOL_DIM + Q_LORA + KV_LORA) // LANE
    return pl.pallas_call(
        partial(_kvprep_kernel, half=QK_ROPE // 4),
        grid=(t // tm,),
        in_specs=[pl.BlockSpec((tm, KV_LORA), lambda i: (i, kva_blk)),
                  pl.BlockSpec((tm, LANE), lambda i: (i, kpe_blk)),
                  pl.BlockSpec((1, KV_LORA), lambda i: (0, 0)),
                  pl.BlockSpec((tm, LANE), lambda i: (i, 0)),
                  pl.BlockSpec((tm, LANE), lambda i: (i, 0))],
        out_specs=[pl.BlockSpec((tm, KV_LORA), lambda i: (i, 0)),
                   pl.BlockSpec((tm, LANE), lambda i: (i, 0))],
        out_shape=[jax.ShapeDtypeStruct((t, KV_LORA), F32), jax.ShapeDtypeStruct((t, LANE), F32)],
        compiler_params=_cparams(("parallel",)),
        name="kvprep",
    )(ucd, ucd, kv_a_norm.reshape(1, KV_LORA), cos, sin)


def _kvexp_kernel(ckv_ref, kpe_ref, wk_ref, wv_ref, k_ref, v_ref):
    ckv = ckv_ref[0]
    kn = jnp.dot(ckv, wk_ref[...].astype(BF16), preferred_element_type=F32)
    k_ref[0, 0] = jnp.concatenate([kn.astype(BF16), kpe_ref[0]], axis=1)
    v_ref[0, 0] = jnp.dot(ckv, wv_ref[...].astype(BF16), preferred_element_type=F32).astype(BF16)


def _kvexp(ckv_all, kpe_all, w_kvb):
    b, l, _ = ckv_all.shape
    return pl.pallas_call(
        _kvexp_kernel,
        grid=(b, MLA_HEADS),
        in_specs=[pl.BlockSpec((1, l, KV_LORA), lambda i, h: (i, 0, 0)),
                  pl.BlockSpec((1, l, LANE), lambda i, h: (i, 0, 0)),
                  pl.BlockSpec((KV_LORA, QK_NOPE), lambda i, h: (0, 2 * h)),
                  pl.BlockSpec((KV_LORA, V_HEAD), lambda i, h: (0, 2 * h + 1))],
        out_specs=[pl.BlockSpec((1, 1, l, MLA_QW), lambda i, h: (i, h, 0, 0)),
                   pl.BlockSpec((1, 1, l, V_HEAD), lambda i, h: (i, h, 0, 0))],
        out_shape=[jax.ShapeDtypeStruct((b, MLA_HEADS, l, MLA_QW), BF16),
                   jax.ShapeDtypeStruct((b, MLA_HEADS, l, V_HEAD), BF16)],
        compiler_params=_cparams(("parallel", "arbitrary")),
        name="kvexp",
    )(ckv_all, kpe_all, w_kvb, w_kvb)


def _mla_kernel(q_ref, k_ref, v_ref, o_ref):
    s = lax.dot_general(q_ref[...], k_ref[0, 0], (((1,), (1,)), ((), ())),
                        preferred_element_type=F32) * MLA_SCALE
    m = jnp.max(s, axis=-1, keepdims=True)
    p = jnp.exp(s - m)
    den = jnp.sum(p, axis=-1, keepdims=True)
    o = jnp.dot(p.astype(BF16), v_ref[0, 0], preferred_element_type=F32)
    o_ref[...] = (o / den).astype(o_ref.dtype)


def _mla(q, kf, vf, *, seq, nseq, row0, tq):
    nq = seq // tq
    base = row0 * nq
    l = kf.shape[2]
    return pl.pallas_call(
        _mla_kernel,
        grid=(nseq, MLA_HEADS, nq),
        in_specs=[pl.BlockSpec((tq, MLA_QW), lambda b, h, i: (base + b * nq + i, h)),
                  pl.BlockSpec((1, 1, l, MLA_QW), lambda b, h, i: (b, h, 0, 0)),
                  pl.BlockSpec((1, 1, l, V_HEAD), lambda b, h, i: (b, h, 0, 0))],
        out_specs=pl.BlockSpec((tq, V_HEAD), lambda b, h, i: (b * nq + i, h)),
        out_shape=jax.ShapeDtypeStruct((nseq * seq, MLA_HEADS * V_HEAD), BF16),
        compiler_params=_cparams(("parallel", "parallel", "arbitrary")),
        name="mla",
    )(q, kf, vf)


def _router_kernel(h_ref, w_ref, b_ref, o_ref):
    w = w_ref[...]
    w_hi = w.astype(BF16)
    w_lo = (w - w_hi.astype(F32)).astype(BF16)
    h = h_ref[...]
    o_ref[...] = (jnp.dot(h, w_hi, preferred_element_type=F32)
                  + jnp.dot(h, w_lo, preferred_element_type=F32) + b_ref[...])


def _router(h, w_router, b_router, tm=ROW_TILE):
    t, d = h.shape
    return pl.pallas_call(
        _router_kernel,
        grid=(t // tm,),
        in_specs=[pl.BlockSpec((tm, d), lambda i: (i, 0)),
                  pl.BlockSpec((d, N_EXPERTS), lambda i: (0, 0)),
                  pl.BlockSpec((1, N_EXPERTS), lambda i: (0, 0))],
        out_specs=pl.BlockSpec((tm, N_EXPERTS), lambda i: (i, 0)),
        out_shape=jax.ShapeDtypeStruct((t, N_EXPERTS), F32),
        compiler_params=_cparams(("parallel",)),
        name="router",
    )(h, w_router, b_router.reshape(1, N_EXPERTS))


def _gmm1_kernel(be_ref, nu_ref, x_ref, wg_ref, wl_ref, bg_ref, bl_ref, o_ref):
    b = pl.program_id(0)

    @pl.when(b < nu_ref[0])
    def _():
        x = x_ref[...]
        glu = jnp.dot(x, wg_ref[0], preferred_element_type=F32) + bg_ref[0]
        lin = jnp.dot(x, wl_ref[0], preferred_element_type=F32) + bl_ref[0]
        glu = jnp.minimum(glu, SWIGLU_LIMIT)
        lin = jnp.clip(lin, -SWIGLU_LIMIT, SWIGLU_LIMIT)
        o_ref[...] = (glu * jax.nn.sigmoid(SWIGLU_ALPHA * glu) * (lin + 1.0)).astype(o_ref.dtype)

    @pl.when(b >= nu_ref[0])
    def _():
        o_ref[...] = jnp.zeros_like(o_ref)


def _gmm1(blk_e, n_used, xs, wg, wl, bg, bl, tn):
    p, d = xs.shape
    e, _, f = wg.shape
    tm = MOE_TM
    grid_spec = pltpu.PrefetchScalarGridSpec(
        num_scalar_prefetch=2,
        grid=(p // tm, f // tn),
        in_specs=[pl.BlockSpec((tm, d), lambda b, j, be, nu: (b, 0)),
                  pl.BlockSpec((1, d, tn), lambda b, j, be, nu: (be[b], 0, j)),
                  pl.BlockSpec((1, d, tn), lambda b, j, be, nu: (be[b], 0, j)),
                  pl.BlockSpec((1, 1, tn), lambda b, j, be, nu: (be[b], 0, j)),
                  pl.BlockSpec((1, 1, tn), lambda b, j, be, nu: (be[b], 0, j))],
        out_specs=pl.BlockSpec((tm, tn), lambda b, j, be, nu: (b, j)),
    )
    return pl.pallas_call(
        _gmm1_kernel, grid_spec=grid_spec,
        out_shape=jax.ShapeDtypeStruct((p, f), BF16),
        compiler_params=_cparams(("parallel", "arbitrary")),
        name="gmm1",
    )(blk_e, n_used, xs, wg, wl, bg.reshape(e, 1, f), bl.reshape(e, 1, f))


def _gmm2_kernel(be_ref, nu_ref, a_ref, w_ref, b_ref, rw_ref, o_ref):
    b = pl.program_id(0)

    @pl.when(b < nu_ref[0])
    def _():
        y = jnp.dot(a_ref[...], w_ref[0], preferred_element_type=F32) + b_ref[0]
        o_ref[...] = y * rw_ref[...]

    @pl.when(b >= nu_ref[0])
    def _():
        o_ref[...] = jnp.zeros_like(o_ref)


def _gmm2(blk_e, n_used, act, w_dn, b_dn, row_w, tn):
    p, f = act.shape
    e, _, d = w_dn.shape
    tm = MOE_TM
    grid_spec = pltpu.PrefetchScalarGridSpec(
        num_scalar_prefetch=2,
        grid=(p // tm, d // tn),
        in_specs=[pl.BlockSpec((tm, f), lambda b, j, be, nu: (b, 0)),
                  pl.BlockSpec((1, f, tn), lambda b, j, be, nu: (be[b], 0, j)),
                  pl.BlockSpec((1, 1, tn), lambda b, j, be, nu: (be[b], 0, j)),
                  pl.BlockSpec((tm, 1), lambda b, j, be, nu: (b, 0))],
        out_specs=pl.BlockSpec((tm, tn), lambda b, j, be, nu: (b, j)),
    )
    return pl.pallas_call(
        _gmm2_kernel, grid_spec=grid_spec,
        out_shape=jax.ShapeDtypeStruct((p, d), F32),
        compiler_params=_cparams(("parallel", "arbitrary")),
        name="gmm2",
    )(blk_e, n_used, act, w_dn, b_dn.reshape(e, 1, d), row_w.reshape(p, 1))


def _moe(h, w_router, b_router, wg, wl, bg, bl, w_dn, b_dn):
    t, d = h.shape
    tm = MOE_TM
    logits = _router(h, w_router, b_router)
    top_logits, top_idx = lax.top_k(logits, TOP_K)
    top_w = jax.nn.softmax(top_logits, axis=-1)
    a = t * TOP_K
    flat_e = top_idx.reshape(-1)
    order = jnp.argsort(flat_e)
    sorted_e = flat_e[order]
    counts = jnp.bincount(flat_e, length=N_EXPERTS)
    padded = (counts + tm - 1) // tm * tm
    pad_end = jnp.cumsum(padded)
    pad_start = pad_end - padded
    start = jnp.cumsum(counts) - counts
    dest = (pad_start[sorted_e] + jnp.arange(a) - start[sorted_e]).astype(jnp.int32)
    n_blocks = -(-a // tm) + N_EXPERTS
    p = n_blocks * tm
    src = jnp.full((p,), t, jnp.int32).at[dest].set((order // TOP_K).astype(jnp.int32))
    row_w = jnp.zeros((p,), F32).at[dest].set(top_w.reshape(-1)[order])
    blk_e = jnp.minimum(jnp.searchsorted(pad_end, jnp.arange(n_blocks) * tm, side='right'),
                        N_EXPERTS - 1).astype(jnp.int32)
    n_used = (pad_end[-1:] // tm).astype(jnp.int32)
    pos = jnp.zeros((a,), jnp.int32).at[order].set(dest).reshape(t, TOP_K)

    xs = jnp.concatenate([h, jnp.zeros((1, d), h.dtype)], axis=0)[src]
    f = wg.shape[-1]
    act = _gmm1(blk_e, n_used, xs, wg, wl, bg, bl, tn=min(512, f))
    yb = _gmm2(blk_e, n_used, act, w_dn, b_dn, row_w, tn=min(1024, d))
    out = yb[pos[:, 0]]
    for kk in range(1, TOP_K):
        out = out + yb[pos[:, kk]]
    return out


def _tile_n(n, pref):
    for t in pref:
        if n % t == 0:
            return t
    return n


def kernel(x_prompt, x_sample, cache_swa_k, cache_swa_v, cache_mla_ckv, cache_mla_kpe, c, c_ctx, w_ada, b_ada, g_pre_mix, g_post_mix, g_pre_ffn, g_post_ffn, w_in_ab, conv_w, sink, w_out_ab, w_in_cd, q_a_norm, w_qb, kv_a_norm, w_kvb, w_pool, pool_scale, w_out_cd, w_router, b_router, w_gu, b_gu, w_dn, b_dn):
    nb_c, s_c, d = x_prompt.shape
    nb_l, s_l, _ = x_sample.shape
    depth = w_ada.shape[0]
    t_c, t_l = nb_c * s_c, nb_l * s_l
    t = t_c + t_l
    assert t_c % s_l == 0 and s_c % EW_TILE == 0 and s_l % ROW_TILE == 0 and t_c % ROW_TILE == 0
    lat_row0 = t_c // s_l

    x = jnp.concatenate([x_prompt.reshape(t_c, d), x_sample.reshape(t_l, d)], axis=0)

    n_cond = 1 + nb_l
    cond = jnp.concatenate([c_ctx[None], c, jnp.zeros((-n_cond % 8, d), F32)], axis=0)
    mods_all = _ada(cond, w_ada, b_ada)

    def mod_index(tile):
        nctx = t_c // tile
        per = s_l // tile
        return lambda i: jnp.where(i < nctx, 0, 1 + (i - nctx) // per)

    cos_s, sin_s = _rope_tables(t_c, s_l, nb_l, HEAD_DIM)
    cos_m, sin_m = _rope_tables(t_c, s_l, nb_l, QK_ROPE)

    new_k, new_v, new_ckv, new_kpe = [], [], [], []
    h_ffn = None
    for l in range(depth):
        j = l // 2
        mods = mods_all[l].reshape(-1, 1, 6 * d)
        if l % 2 == 0:
            w_in = w_in_ab[j].astype(BF16)
            u = _mm(x, w_in, tn=_tile_n(w_in.shape[1], (512, 256, 128)), out_dtype=F32,
                    g=g_pre_mix[l], mods=mods, mod_cols=(1, 0), mod_index=mod_index(ROW_TILE))
            tw = _tile_n(3 * CONV_DIM, (1024, 512, 256, 128))
            qk = _rope(u, cos_s, sin_s, col0=3 * CONV_DIM // tw, width=Q_DIM + KV_DIM,
                       half=HEAD_DIM // 4, tw=tw)
            ya_c = _conv(u, conv_w[j], seq=s_c, nseq=nb_c, row0=0)
            ya_l = _conv(u, conv_w[j], seq=s_l, nseq=nb_l, row0=lat_row0)
            yb_c = _swa_context(qk, u, sink[j], seq=s_c, nseq=nb_c)
            past = cache_swa_k.shape[2]
            yb_l = _swa_latent(qk, u, cache_swa_k[:, j].reshape(nb_l, past, KV_DIM),
                               cache_swa_v[:, j].reshape(nb_l, past, KV_DIM), sink[j],
                               seq=s_l, nseq=nb_l, row0=lat_row0)
            kv0 = 3 * CONV_DIM + Q_DIM
            new_k.append(u[:t_c, kv0:kv0 + KV_DIM].reshape(nb_c, s_c, SWA_KV_HEADS, HEAD_DIM))
            new_v.append(u[:t_c, kv0 + KV_DIM:].reshape(nb_c, s_c, SWA_KV_HEADS, HEAD_DIM))
            mix = jnp.concatenate([jnp.concatenate([ya_c, yb_c], axis=1),
                                   jnp.concatenate([ya_l, yb_l], axis=1)], axis=0)
            w_out = w_out_ab[j].astype(BF16)
        else:
            cd_in = w_in_cd.shape[-1]
            cd_pad = -cd_in % LANE
            w_in = jnp.pad(w_in_cd[j], ((0, 0), (0, cd_pad))).astype(BF16)
            ucd = _mm(x, w_in, tn=_tile_n(w_in.shape[1], (896, 384, 128)), out_dtype=F32,
                      g=g_pre_mix[l], mods=mods, mod_cols=(1, 0), mod_index=mod_index(ROW_TILE))
            wq = w_qb[j].reshape(Q_LORA, MLA_HEADS, QK_NOPE + QK_ROPE)
            wq = jnp.pad(wq, ((0, 0), (0, 0), (0, MLA_QW - QK_NOPE - QK_ROPE)))
            wq = wq.reshape(Q_LORA, MLA_HEADS * MLA_QW).astype(BF16)
            q = _mm(ucd, wq, tn=_tile_n(wq.shape[1], (1536, 1024, 512, 256)), out_dtype=F32,
                    g=q_a_norm[j], x_col=POOL_DIM // Q_LORA)
            cos_q = jnp.concatenate([jnp.ones_like(cos_m), cos_m], axis=1)
            sin_q = jnp.concatenate([jnp.zeros_like(sin_m), sin_m], axis=1)
            q = _rope(q, cos_q, sin_q, col0=0, width=MLA_HEADS * MLA_QW, half=QK_ROPE // 4,
                      tw=_tile_n(MLA_HEADS * MLA_QW, (1024, 512, 256)))
            ckv, kpe = _kvprep(ucd, kv_a_norm[j], cos_m, sin_m)
            new_ckv.append(ckv[:t_c].reshape(nb_c, s_c, KV_LORA))
            new_kpe.append(kpe[:t_c, :QK_ROPE].reshape(nb_c, s_c, QK_ROPE))
            lane_ok = (jnp.arange(LANE) < QK_ROPE)
            kpe_b = jnp.where(lane_ok, kpe, 0.0).astype(BF16)
            ckv_b = ckv.astype(BF16)
            kf_c, vf_c = _kvexp(ckv_b[:t_c].reshape(nb_c, s_c, KV_LORA), kpe_b[:t_c].reshape(nb_c, s_c, LANE),
                                w_kvb[j])
            yd_c = _mla(q, kf_c, vf_c, seq=s_c, nseq=nb_c, row0=0, tq=s_c)
            cache_kpe = jnp.pad(cache_mla_kpe[:, j], ((0, 0), (0, 0), (0, LANE - QK_ROPE))).astype(BF16)
            ckv_all = jnp.concatenate([cache_mla_ckv[:, j].astype(BF16),
                                       ckv_b[t_c:].reshape(nb_l, s_l, KV_LORA)], axis=1)
            kpe_all = jnp.concatenate([cache_kpe, kpe_b[t_c:].reshape(nb_l, s_l, LANE)], axis=1)
            kf_l, vf_l = _kvexp(ckv_all, kpe_all, w_kvb[j])
            yd_l = _mla(q, kf_l, vf_l, seq=s_l, nseq=nb_l, row0=lat_row0, tq=min(256, s_l))
            yc_c = _pool(ucd, w_pool[j], pool_scale[j], seq=s_c, nseq=nb_c, row0=0)
            yc_l = _pool(ucd, w_pool[j], pool_scale[j], seq=s_l, nseq=nb_l, row0=lat_row0)
            mix = jnp.concatenate([jnp.concatenate([yc_c, yd_c], axis=1),
                                   jnp.concatenate([yc_l, yd_l], axis=1)], axis=0)
            w_out = w_out_cd[j].astype(BF16)

        y = _mm(mix, w_out, tn=_tile_n(d, (512, 256, 128)), out_dtype=F32)
        x, h_ffn = _resid(x, y, mods, mod_index(EW_TILE), 2, g_post_mix[l],
                          gpre=g_pre_ffn[l], mod_cols=(4, 3))

        wg = w_gu[l][:, :, 0::2].astype(BF16)
        wl = w_gu[l][:, :, 1::2].astype(BF16)
        fo = _moe(h_ffn, w_router[l], b_router[l], wg, wl, b_gu[l][:, 0::2], b_gu[l][:, 1::2],
                  w_dn[l].astype(BF16), b_dn[l])
        x = _resid(x, fo, mods, mod_index(EW_TILE), 5, g_post_ffn[l])

    y_prompt = x[:t_c].reshape(nb_c, s_c, d)
    y_sample = x[t_c:].reshape(nb_l, s_l, d)
    return (y_prompt, y_sample, jnp.stack(new_k, axis=1), jnp.stack(new_v, axis=1),
            jnp.stack(new_ckv, axis=1), jnp.stack(new_kpe, axis=1))
```
